```python
import jax, jax.numpy as jnp
from jax import lax
import numpy as np

D_MODEL = 2048
BATCH = 4
SEQ = 2048
DEPTH = 1
DEC_BATCH = 128
DEC_SEQ = 4
PAST_LEN = 16384
PAGE_SIZE = 128

D_RNN = D_MODEL // 2
RNN_HEADS = 8
RNN_HEAD_DIM = D_RNN // RNN_HEADS
RNN_CONV_W = 4
LRU_C = 8.0
D_CONV = D_MODEL // 2
CONV_W = 31
N_KEYS = 128
N_EXPERTS = N_KEYS * N_KEYS
PEER_HEADS = 8
PEER_TOPK = 16
D_KEY = 256
D_HALF = D_KEY // 2
PEER_BLOCK = 128
EPS = 1e-6
IN_COLS = 2 * D_RNN + 2 * D_CONV + 2 * D_MODEL

kernel_name = 'hawk_conformer_peer_hybrid_step'


def _rmsnorm(x, g):
    xf = x.astype(jnp.float32)
    r = xf * lax.rsqrt(jnp.mean(xf * xf, axis=-1, keepdims=True) + EPS)
    return (r * g.astype(jnp.float32)).astype(x.dtype)


def _layernorm(x, g, b):
    xf = x.astype(jnp.float32)
    mu = jnp.mean(xf, axis=-1, keepdims=True)
    xc = xf - mu
    var = jnp.mean(xc * xc, axis=-1, keepdims=True)
    return (xc * lax.rsqrt(var + EPS) * g.astype(jnp.float32) + b.astype(jnp.float32)).astype(x.dtype)


def _split_cols(z):
    idx = [D_RNN, 2 * D_RNN, 2 * D_RNN + D_CONV, 2 * D_RNN + 2 * D_CONV,
           2 * D_RNN + 2 * D_CONV + D_MODEL]
    return jnp.split(z, idx, axis=-1)


def _causal_dwconv(x, buf, w, b):
    width, c = w.shape
    xp = jnp.concatenate([buf.astype(x.dtype), x], axis=1)
    y = lax.conv_general_dilated(xp, w.astype(x.dtype)[:, None, :], window_strides=(1,),
                                 padding='VALID', dimension_numbers=('NWC', 'WIO', 'NWC'),
                                 feature_group_count=c)
    return y + b.astype(x.dtype), xp[:, xp.shape[1] - (width - 1):]


def _rglru(xc, h0, w_r, b_r, w_i, b_i, lam):
    bsz, t, _ = xc.shape
    f32 = jnp.float32
    xf = xc.astype(f32)
    xh = xf.reshape(bsz, t, RNN_HEADS, RNN_HEAD_DIM)
    r = jax.nn.sigmoid(jnp.einsum('bthi,hij->bthj', xh, w_r.astype(f32)).reshape(bsz, t, D_RNN) + b_r.astype(f32))
    ig = jax.nn.sigmoid(jnp.einsum('bthi,hij->bthj', xh, w_i.astype(f32)).reshape(bsz, t, D_RNN) + b_i.astype(f32))
    log_a = -LRU_C * jax.nn.softplus(-lam.astype(f32)) * r
    a = jnp.exp(log_a)
    bterm = jnp.sqrt(-jnp.expm1(2.0 * log_a)) * (ig * xf)

    def combine(lhs, rhs):
        a1, b1 = lhs
        a2, b2 = rhs
        return a1 * a2, a2 * b1 + b2

    a_cum, b_cum = lax.associative_scan(combine, (a, bterm), axis=1)
    h = a_cum * h0.astype(f32)[:, None, :] + b_cum
    return h, h[:, -1]


def _peer(xn, w_q, keys1, keys2, u_tab, v_tab):
    bsz, t, d = xn.shape
    n = bsz * t
    nb = -(-n // PEER_BLOCK)
    flat = jnp.pad(xn.reshape(n, d), ((0, nb * PEER_BLOCK - n), (0, 0))).reshape(nb, PEER_BLOCK, d)
    f32 = jnp.float32

    def block(xb):
        q = (xb @ w_q).astype(f32).reshape(PEER_BLOCK, PEER_HEADS, 2, D_HALF)
        s1 = jnp.einsum('thd,hkd->thk', q[:, :, 0], keys1.astype(f32))
        s2 = jnp.einsum('thd,hkd->thk', q[:, :, 1], keys2.astype(f32))
        v1, i1 = lax.top_k(s1, PEER_TOPK)
        v2, i2 = lax.top_k(s2, PEER_TOPK)
        cand = (v1[..., :, None] + v2[..., None, :]).reshape(PEER_BLOCK, PEER_HEADS, PEER_TOPK * PEER_TOPK)
        sv, si = lax.top_k(cand, PEER_TOPK)
        e = (jnp.take_along_axis(i1, si // PEER_TOPK, axis=-1) * N_KEYS
             + jnp.take_along_axis(i2, si % PEER_TOPK, axis=-1))
        g = jax.nn.softmax(sv, axis=-1)
        u = u_tab[e]
        act = jax.nn.gelu(jnp.einsum('thkd,td->thk', u, xb).astype(f32))
        v = v_tab[e]
        return jnp.einsum('thk,thkd->td', (g * act).astype(xb.dtype), v)

    out = lax.map(block, flat).reshape(nb * PEER_BLOCK, d)[:n]
    return out.reshape(bsz, t, d)


def _trunk(x, rnn_conv, rnn_h, cm_conv, p):
    new_rc, new_h, new_cc = [], [], []
    for l in range(DEPTH):
        xn = _rmsnorm(x, p['g_mix'][l])
        xr, yr, ca, cb, ga, gb = _split_cols(xn @ p['w_in'][l])
        xc, rc = _causal_dwconv(xr, rnn_conv[l], p['w_rconv'][l], p['b_rconv'][l])
        h, h_last = _rglru(xc, rnn_h[l], p['w_r'][l], p['b_r'][l], p['w_i'][l], p['b_i'][l], p['lam'][l])
        branch_a = (h.astype(x.dtype) * jax.nn.gelu(yr)) @ p['w_a'][l]
        glu = ca * jax.nn.sigmoid(cb)
        dc, cc = _causal_dwconv(glu, cm_conv[l], p['w_cconv'][l], p['b_cconv'][l])
        branch_b = jax.nn.silu(_layernorm(dc, p['ln_g'][l], p['ln_b'][l])) @ p['w_b'][l]
        merged = jax.nn.sigmoid(ga) * branch_a + jax.nn.sigmoid(gb) * branch_b
        x = x + merged @ p['w_out'][l]
        x = x + _peer(_rmsnorm(x, p['g_ffn'][l]), p['w_q'][l], p['keys1'][l], p['keys2'][l],
                      p['u_tab'][l], p['v_tab'][l])
        new_rc.append(rc.astype(rnn_conv.dtype))
        new_h.append(h_last.astype(rnn_h.dtype))
        new_cc.append(cc.astype(cm_conv.dtype))
    y = _rmsnorm(x, p['g_final'])
    return y, jnp.stack(new_rc), jnp.stack(new_h), jnp.stack(new_cc)


def setup_inputs(seed: int = 0) -> dict:
    key = jax.random.key(seed)
    ks = jax.random.split(key, 32)
    nrm = jax.random.normal
    f32 = jnp.float32
    u_lam = jax.random.uniform(ks[10], (DEPTH, D_RNN), f32, 0.9, 0.999)
    sig_lam = u_lam ** (1.0 / LRU_C)
    lam = jnp.log(sig_lam) - jnp.log1p(-sig_lam)
    return {
        'x_prompt': nrm(ks[0], (BATCH, SEQ, D_MODEL), f32),
        'x_sample': nrm(ks[1], (DEC_BATCH, DEC_SEQ, D_MODEL), f32),
        'state_rnn_conv': nrm(ks[2], (DEPTH, DEC_BATCH, RNN_CONV_W - 1, D_RNN), f32),
        'state_rnn_h': 0.5 * nrm(ks[3], (DEPTH, DEC_BATCH, D_RNN), f32),
        'state_cm_conv': 0.5 * nrm(ks[4], (DEPTH, DEC_BATCH, CONV_W - 1, D_CONV), f32),
        'g_mix': 1.0 + 0.02 * nrm(ks[5], (DEPTH, D_MODEL), f32),
        'w_in': nrm(ks[6], (DEPTH, D_MODEL, IN_COLS), f32) * D_MODEL ** -0.5,
        'w_rconv': nrm(ks[7], (DEPTH, RNN_CONV_W, D_RNN), f32) * RNN_CONV_W ** -0.5,
        'b_rconv': 0.01 * nrm(ks[8], (DEPTH, D_RNN), f32),
        'w_r': nrm(ks[9], (DEPTH, RNN_HEADS, RNN_HEAD_DIM, RNN_HEAD_DIM), f32) * RNN_HEAD_DIM ** -0.5,
        'b_r': 0.01 * nrm(ks[11], (DEPTH, D_RNN), f32),
        'w_i': nrm(ks[12], (DEPTH, RNN_HEADS, RNN_HEAD_DIM, RNN_HEAD_DIM), f32) * RNN_HEAD_DIM ** -0.5,
        'b_i': 0.01 * nrm(ks[13], (DEPTH, D_RNN), f32),
        'lam': lam,
        'w_a': nrm(ks[14], (DEPTH, D_RNN, D_MODEL), f32) * D_RNN ** -0.5,
        'w_cconv': nrm(ks[15], (DEPTH, CONV_W, D_CONV), f32) * CONV_W ** -0.5,
        'b_cconv': 0.01 * nrm(ks[16], (DEPTH, D_CONV), f32),
        'ln_g': 1.0 + 0.02 * nrm(ks[17], (DEPTH, D_CONV), f32),
        'ln_b': 0.01 * nrm(ks[18], (DEPTH, D_CONV), f32),
        'w_b': nrm(ks[19], (DEPTH, D_CONV, D_MODEL), f32) * D_CONV ** -0.5,
        'w_out': nrm(ks[20], (DEPTH, D_MODEL, D_MODEL), f32) * D_MODEL ** -0.5,
        'g_ffn': 1.0 + 0.02 * nrm(ks[21], (DEPTH, D_MODEL), f32),
        'w_q': nrm(ks[22], (DEPTH, D_MODEL, PEER_HEADS * D_KEY), f32) * D_MODEL ** -0.5,
        'keys1': nrm(ks[23], (DEPTH, PEER_HEADS, N_KEYS, D_HALF), f32) * D_HALF ** -0.5,
        'keys2': nrm(ks[24], (DEPTH, PEER_HEADS, N_KEYS, D_HALF), f32) * D_HALF ** -0.5,
        'u_tab': nrm(ks[25], (DEPTH, N_EXPERTS, D_MODEL), f32) * D_MODEL ** -0.5,
        'v_tab': 0.25 * nrm(ks[26], (DEPTH, N_EXPERTS, D_MODEL), f32),
        'g_final': 1.0 + 0.02 * nrm(ks[27], (D_MODEL,), f32),
    }


def reference(x_prompt, x_sample, state_rnn_conv, state_rnn_h, state_cm_conv,
              g_mix, w_in, w_rconv, b_rconv, w_r, b_r, w_i, b_i, lam, w_a,
              w_cconv, b_cconv, ln_g, ln_b, w_b, w_out, g_ffn, w_q, keys1, keys2,
              u_tab, v_tab, g_final):
    p = {'g_mix': g_mix, 'w_in': w_in, 'w_rconv': w_rconv, 'b_rconv': b_rconv,
         'w_r': w_r, 'b_r': b_r, 'w_i': w_i, 'b_i': b_i, 'lam': lam, 'w_a': w_a,
         'w_cconv': w_cconv, 'b_cconv': b_cconv, 'ln_g': ln_g, 'ln_b': ln_b, 'w_b': w_b,
         'w_out': w_out, 'g_ffn': g_ffn, 'w_q': w_q, 'keys1': keys1, 'keys2': keys2,
         'u_tab': u_tab, 'v_tab': v_tab, 'g_final': g_final}
    bp = x_prompt.shape[0]
    dt = x_prompt.dtype
    p_rc0 = jnp.zeros((DEPTH, bp, RNN_CONV_W - 1, D_RNN), dt)
    p_h0 = jnp.zeros((DEPTH, bp, D_RNN), dt)
    p_cc0 = jnp.zeros((DEPTH, bp, CONV_W - 1, D_CONV), dt)
    y_prompt, p_rc, p_h, p_cc = _trunk(x_prompt, p_rc0, p_h0, p_cc0, p)
    y_sample, s_rc, s_h, s_cc = _trunk(x_sample, state_rnn_conv, state_rnn_h, state_cm_conv, p)
    return (y_prompt, y_sample, p_rc, p_h, p_cc, s_rc, s_h, s_cc)
```

```python
import functools

import numpy as np
import jax
import jax.numpy as jnp
from jax import lax
from jax.experimental import pallas as pl
from jax.experimental.pallas import tpu as pltpu

F32 = jnp.float32
BF16 = jnp.bfloat16

D_MODEL = 2048
D_RNN = 1024
RNN_HEADS = 8
RNN_HEAD_DIM = 128
RNN_CONV_W = 4
LRU_C = 8.0
D_CONV = 1024
CONV_W = 31
N_KEYS = 128
PEER_HEADS = 8
PEER_TOPK = 16
D_HALF = 128
EPS = 1e-6
IN_COLS = 2 * D_RNN + 2 * D_CONV + 2 * D_MODEL

SUBLANES = 8
LANES = 128
VMEM_LIMIT = 56 * 1024 * 1024

NT_DIMS = (((1,), (1,)), ((), ()))


def _cparams(*sem):
    return pltpu.CompilerParams(dimension_semantics=sem, vmem_limit_bytes=VMEM_LIMIT)


def _gelu(x):
    return jax.nn.gelu(x)


def _sigmoid(x):
    return jax.nn.sigmoid(x)


def _inproj_kernel(x_ref, g_ref, w_ref, o_ref, xn_ref, *, rows):
    @pl.when(pl.program_id(1) == 0)
    def _():
        def body(i, c):
            r0 = pl.multiple_of(i * 16, 16)
            x = x_ref[pl.ds(r0, 16), :]
            ms = jnp.mean(x * x, axis=-1, keepdims=True)
            xn_ref[pl.ds(r0, 16), :] = (x * lax.rsqrt(ms + EPS) * g_ref[...]).astype(BF16)
            return c
        lax.fori_loop(0, rows // 16, body, 0)

    o_ref[...] = jnp.dot(xn_ref[...], w_ref[...], preferred_element_type=F32)


def _inproj(x2d, g_mix, w_in_bf):
    n = x2d.shape[0]
    tm = min(n, 1024)
    tn = 512
    return pl.pallas_call(
        functools.partial(_inproj_kernel, rows=tm),
        grid=(n // tm, IN_COLS // tn),
        in_specs=[
            pl.BlockSpec((tm, D_MODEL), lambda i, j: (i, 0)),
            pl.BlockSpec((1, D_MODEL), lambda i, j: (0, 0)),
            pl.BlockSpec((D_MODEL, tn), lambda i, j: (0, j)),
        ],
        out_specs=pl.BlockSpec((tm, tn), lambda i, j: (i, j)),
        out_shape=jax.ShapeDtypeStruct((n, IN_COLS), F32),
        scratch_shapes=[pltpu.VMEM((tm, D_MODEL), BF16)],
        compiler_params=_cparams("parallel", "arbitrary"),
        name="inproj",
    )(x2d, g_mix, w_in_bf)


def _lru_gates(xc, wr_ref, br_ref, wi_ref, bi_ref, lam_ref):
    nl = -lam_ref[...]
    c = -LRU_C * (jnp.maximum(nl, 0.0) + jnp.log1p(jnp.exp(-jnp.abs(nl))))
    a_parts, b_parts = [], []
    for h in range(RNN_HEADS):
        sl = slice(h * RNN_HEAD_DIM, (h + 1) * RNN_HEAD_DIM)
        xh = xc[:, sl]
        xb = xh.astype(BF16)
        r = _sigmoid(jnp.dot(xb, wr_ref[h], preferred_element_type=F32) + br_ref[:, sl])
        ig = _sigmoid(jnp.dot(xb, wi_ref[h], preferred_element_type=F32) + bi_ref[:, sl])
        log_a = c[:, sl] * r
        a = jnp.exp(log_a)
        a_parts.append(a)
        b_parts.append(jnp.sqrt(-jnp.tanh(log_a) * (a * a + 1.0)) * (ig * xh))
    return jnp.concatenate(a_parts, axis=1), jnp.concatenate(b_parts, axis=1)


def _rnn_prompt_kernel(xr_ref, yr_ref, wc_ref, bc_ref, wr_ref, br_ref, wi_ref, bi_ref, lam_ref,
                       ua_ref, rc_ref, hl_ref, xbuf, a_s, b_s, hcar, *, tt_rows, chunk):
    tt = pl.program_id(1)
    ntt = pl.num_programs(1)

    @pl.when(tt == 0)
    def _():
        xbuf[0:SUBLANES, :] = jnp.zeros((SUBLANES, D_RNN), F32)
        hcar[...] = jnp.zeros((1, D_RNN), F32)

    xbuf[SUBLANES:, :] = xr_ref[...]

    for ci in range(tt_rows // chunk):
        r0 = ci * chunk
        xc = jnp.broadcast_to(bc_ref[...], (chunk, D_RNN))
        for k in range(RNN_CONV_W):
            off = SUBLANES - (RNN_CONV_W - 1) + k + r0
            xc = xc + wc_ref[k:k + 1, :] * xbuf[off:off + chunk, :]
        a, b = _lru_gates(xc, wr_ref, br_ref, wi_ref, bi_ref, lam_ref)
        a_s[r0:r0 + chunk, :] = a
        b_s[r0:r0 + chunk, :] = b

    row = lax.broadcasted_iota(jnp.int32, (SUBLANES, D_RNN), 0)

    def scan_body(i, hprev):
        base = pl.multiple_of(i * 16, 16)
        hs = []
        for half in range(2):
            r0 = base + half * SUBLANES
            a = a_s[pl.ds(r0, SUBLANES), :]
            b = b_s[pl.ds(r0, SUBLANES), :]
            for s in (1, 2, 4):
                a_sh = jnp.where(row >= s, pltpu.roll(a, s, 0), 1.0)
                b_sh = jnp.where(row >= s, pltpu.roll(b, s, 0), 0.0)
                b = a * b_sh + b
                a = a * a_sh
            h = a * hprev + b
            hprev = h[SUBLANES - 1:SUBLANES, :]
            hs.append(h)
        h16 = jnp.concatenate(hs, axis=0)
        ua_ref[pl.ds(base, 16), :] = (h16 * _gelu(yr_ref[pl.ds(base, 16), :])).astype(BF16)
        return hprev

    hlast = lax.fori_loop(0, tt_rows // 16, scan_body, hcar[...])
    hcar[...] = hlast
    xbuf[0:SUBLANES, :] = xbuf[tt_rows:tt_rows + SUBLANES, :]

    @pl.when(tt == ntt - 1)
    def _():
        rc_ref[0] = xbuf[SUBLANES - (RNN_CONV_W - 1):SUBLANES, :]
        hl_ref[0] = hlast


def _rnn_prompt(z, nb, t, p):
    tt_rows = 256
    ntt = t // tt_rows
    full = lambda shape: pl.BlockSpec(shape, lambda b, i: (0,) * len(shape))
    return pl.pallas_call(
        functools.partial(_rnn_prompt_kernel, tt_rows=tt_rows, chunk=128),
        grid=(nb, ntt),
        in_specs=[
            pl.BlockSpec((tt_rows, D_RNN), lambda b, i: (b * ntt + i, 0)),
            pl.BlockSpec((tt_rows, D_RNN), lambda b, i: (b * ntt + i, 1)),
            full((RNN_CONV_W, D_RNN)), full((1, D_RNN)),
            full((RNN_HEADS, RNN_HEAD_DIM, RNN_HEAD_DIM)), full((1, D_RNN)),
            full((RNN_HEADS, RNN_HEAD_DIM, RNN_HEAD_DIM)), full((1, D_RNN)),
            full((1, D_RNN)),
        ],
        out_specs=[
            pl.BlockSpec((tt_rows, D_RNN), lambda b, i: (b * ntt + i, 0)),
            pl.BlockSpec((1, RNN_CONV_W - 1, D_RNN), lambda b, i: (b, 0, 0)),
            pl.BlockSpec((1, 1, D_RNN), lambda b, i: (b, 0, 0)),
        ],
        out_shape=[
            jax.ShapeDtypeStruct((nb * t, D_RNN), BF16),
            jax.ShapeDtypeStruct((nb, RNN_CONV_W - 1, D_RNN), F32),
            jax.ShapeDtypeStruct((nb, 1, D_RNN), F32),
        ],
        scratch_shapes=[
            pltpu.VMEM((tt_rows + SUBLANES, D_RNN), F32),
            pltpu.VMEM((tt_rows, D_RNN), F32),
            pltpu.VMEM((tt_rows, D_RNN), F32),
            pltpu.VMEM((1, D_RNN), F32),
        ],
        compiler_params=_cparams("arbitrary", "arbitrary"),
        name="rnn_prompt",
    )(z, z, p["w_rconv"], p["b_rconv"], p["w_r"], p["b_r"], p["w_i"], p["b_i"], p["lam"])


CONV_PAD = 32


def _layernorm_silu(dc, g, b):
    mu = jnp.mean(dc, axis=-1, keepdims=True)
    xc = dc - mu
    var = jnp.mean(xc * xc, axis=-1, keepdims=True)
    y = xc * lax.rsqrt(var + EPS) * g + b
    return y * _sigmoid(y)


def _conv_prompt_kernel(ca_ref, cb_ref, wc_ref, bc_ref, lg_ref, lb_ref, ub_ref, cc_ref, gbuf,
                        *, tt_rows, chunk):
    tt = pl.program_id(1)
    ntt = pl.num_programs(1)

    @pl.when(tt == 0)
    def _():
        gbuf[0:CONV_PAD, :] = jnp.zeros((CONV_PAD, D_CONV), F32)

    for ci in range(tt_rows // 64):
        r0 = ci * 64
        gbuf[CONV_PAD + r0:CONV_PAD + r0 + 64, :] = (
            ca_ref[r0:r0 + 64, :] * _sigmoid(cb_ref[r0:r0 + 64, :]))

    for ci in range(tt_rows // chunk):
        r0 = ci * chunk
        acc = jnp.broadcast_to(bc_ref[...], (chunk, D_CONV))
        for k in range(CONV_W):
            off = CONV_PAD - (CONV_W - 1) + k + r0
            acc = acc + wc_ref[k:k + 1, :] * gbuf[off:off + chunk, :]
        ub_ref[r0:r0 + chunk, :] = _layernorm_silu(acc, lg_ref[...], lb_ref[...]).astype(BF16)

    @pl.when(tt == ntt - 1)
    def _():
        cc_ref[0] = gbuf[tt_rows + CONV_PAD - (CONV_W - 1):tt_rows + CONV_PAD, :]

    gbuf[0:CONV_PAD, :] = gbuf[tt_rows:tt_rows + CONV_PAD, :]


def _conv_prompt(z, nb, t, p):
    tt_rows = 256
    ntt = t // tt_rows
    full = lambda shape: pl.BlockSpec(shape, lambda b, i: (0,) * len(shape))
    return pl.pallas_call(
        functools.partial(_conv_prompt_kernel, tt_rows=tt_rows, chunk=16),
        grid=(nb, ntt),
        in_specs=[
            pl.BlockSpec((tt_rows, D_CONV), lambda b, i: (b * ntt + i, 2)),
            pl.BlockSpec((tt_rows, D_CONV), lambda b, i: (b * ntt + i, 3)),
            full((CONV_W, D_CONV)), full((1, D_CONV)), full((1, D_CONV)), full((1, D_CONV)),
        ],
        out_specs=[
            pl.BlockSpec((tt_rows, D_CONV), lambda b, i: (b * ntt + i, 0)),
            pl.BlockSpec((1, CONV_W - 1, D_CONV), lambda b, i: (b, 0, 0)),
        ],
        out_shape=[
            jax.ShapeDtypeStruct((nb * t, D_CONV), BF16),
            jax.ShapeDtypeStruct((nb, CONV_W - 1, D_CONV), F32),
        ],
        scratch_shapes=[pltpu.VMEM((tt_rows + CONV_PAD, D_CONV), F32)],
        compiler_params=_cparams("arbitrary", "arbitrary"),
        name="conv_prompt",
    )(z, z, p["w_cconv"], p["b_cconv"], p["ln_g"], p["ln_b"])


def _seq_sample_kernel(xr_ref, yr_ref, ca_ref, cb_ref, rc0_ref, h0_ref, cc0_ref,
                       wc_ref, bc_ref, wr_ref, br_ref, wi_ref, bi_ref, lam_ref,
                       wcc_ref, bcc_ref, lg_ref, lb_ref,
                       ua_ref, ub_ref, rc_ref, hl_ref, cc_ref, *, steps, sb):
    nrc = RNN_CONV_W - 1
    ncc = CONV_W - 1

    def xp(j):
        if j < nrc:
            return rc0_ref[:, j * D_RNN:(j + 1) * D_RNN]
        return xr_ref[j - nrc]

    h = h0_ref[...]
    for t in range(steps):
        xc = jnp.broadcast_to(bc_ref[...], (sb, D_RNN))
        for k in range(RNN_CONV_W):
            xc = xc + wc_ref[k:k + 1, :] * xp(t + k)
        a, b = _lru_gates(xc, wr_ref, br_ref, wi_ref, bi_ref, lam_ref)
        h = a * h + b
        ua_ref[t] = (h * _gelu(yr_ref[t])).astype(BF16)
    hl_ref[...] = h
    for j in range(nrc):
        rc_ref[:, j * D_RNN:(j + 1) * D_RNN] = xp(steps + j)

    for j in range(ncc - steps):
        cc_ref[:, j * D_CONV:(j + 1) * D_CONV] = cc0_ref[:, (j + steps) * D_CONV:(j + steps + 1) * D_CONV]
    for t in range(steps):
        j = ncc - steps + t
        cc_ref[:, j * D_CONV:(j + 1) * D_CONV] = ca_ref[t] * _sigmoid(cb_ref[t])

    def gp(j):
        if j < steps:
            return cc0_ref[:, j * D_CONV:(j + 1) * D_CONV]
        return cc_ref[:, (j - steps) * D_CONV:(j - steps + 1) * D_CONV]

    for t in range(steps):
        acc = jnp.broadcast_to(bcc_ref[...], (sb, D_CONV))
        for k in range(CONV_W):
            acc = acc + wcc_ref[k:k + 1, :] * gp(t + k)
        ub_ref[t] = _layernorm_silu(acc, lg_ref[...], lb_ref[...]).astype(BF16)


def _seq_sample(z3, rc0, h0, cc0, p):
    steps, nseq, _ = z3.shape
    sb = 32
    full = lambda shape: pl.BlockSpec(shape, lambda i: (0,) * len(shape))
    zspec = lambda col: pl.BlockSpec((steps, sb, D_RNN), lambda i, col=col: (0, i, col))
    nrc, ncc = RNN_CONV_W - 1, CONV_W - 1
    return pl.pallas_call(
        functools.partial(_seq_sample_kernel, steps=steps, sb=sb),
        grid=(nseq // sb,),
        in_specs=[
            zspec(0), zspec(1), zspec(2), zspec(3),
            pl.BlockSpec((sb, nrc * D_RNN), lambda i: (i, 0)),
            pl.BlockSpec((sb, D_RNN), lambda i: (i, 0)),
            pl.BlockSpec((sb, ncc * D_CONV), lambda i: (i, 0)),
            full((RNN_CONV_W, D_RNN)), full((1, D_RNN)),
            full((RNN_HEADS, RNN_HEAD_DIM, RNN_HEAD_DIM)), full((1, D_RNN)),
            full((RNN_HEADS, RNN_HEAD_DIM, RNN_HEAD_DIM)), full((1, D_RNN)),
            full((1, D_RNN)),
            full((CONV_W, D_CONV)), full((1, D_CONV)), full((1, D_CONV)), full((1, D_CONV)),
        ],
        out_specs=[
            pl.BlockSpec((steps, sb, D_RNN), lambda i: (0, i, 0)),
            pl.BlockSpec((steps, sb, D_CONV), lambda i: (0, i, 0)),
            pl.BlockSpec((sb, nrc * D_RNN), lambda i: (i, 0)),
            pl.BlockSpec((sb, D_RNN), lambda i: (i, 0)),
            pl.BlockSpec((sb, ncc * D_CONV), lambda i: (i, 0)),
        ],
        out_shape=[
            jax.ShapeDtypeStruct((steps, nseq, D_RNN), BF16),
            jax.ShapeDtypeStruct((steps, nseq, D_CONV), BF16),
            jax.ShapeDtypeStruct((nseq, nrc * D_RNN), F32),
            jax.ShapeDtypeStruct((nseq, D_RNN), F32),
            jax.ShapeDtypeStruct((nseq, ncc * D_CONV), F32),
        ],
        compiler_params=_cparams("parallel"),
        name="seq_sample",
    )(z3, z3, z3, z3, rc0, h0, cc0,
      p["w_rconv"], p["b_rconv"], p["w_r"], p["b_r"], p["w_i"], p["b_i"], p["lam"],
      p["w_cconv"], p["b_cconv"], p["ln_g"], p["ln_b"])


def _merge_kernel(ua_ref, ub_ref, ga_ref, gb_ref, x_ref, wa_ref, wb_ref, wo_ref, gf_ref,
                  x1_ref, xn_ref):
    ba = jnp.dot(ua_ref[...], wa_ref[...], preferred_element_type=F32)
    bb = jnp.dot(ub_ref[...], wb_ref[...], preferred_element_type=F32)
    merged = _sigmoid(ga_ref[...]) * ba + _sigmoid(gb_ref[...]) * bb
    x1 = x_ref[...] + jnp.dot(merged.astype(BF16), wo_ref[...], preferred_element_type=F32)
    x1_ref[...] = x1
    ms = jnp.mean(x1 * x1, axis=-1, keepdims=True)
    xn_ref[...] = (x1 * lax.rsqrt(ms + EPS) * gf_ref[...]).astype(BF16)


def _merge(ua, ub, z, x2d, p):
    n = x2d.shape[0]
    tm = 256
    const = lambda shape: pl.BlockSpec(shape, lambda i: (0,) * len(shape),
                                       pipeline_mode=pl.Buffered(1))
    return pl.pallas_call(
        _merge_kernel,
        grid=(n // tm,),
        in_specs=[
            pl.BlockSpec((tm, D_RNN), lambda i: (i, 0)),
            pl.BlockSpec((tm, D_CONV), lambda i: (i, 0)),
            pl.BlockSpec((tm, D_MODEL), lambda i: (i, 2)),
            pl.BlockSpec((tm, D_MODEL), lambda i: (i, 3)),
            pl.BlockSpec((tm, D_MODEL), lambda i: (i, 0)),
            const((D_RNN, D_MODEL)), const((D_CONV, D_MODEL)), const((D_MODEL, D_MODEL)),
            const((1, D_MODEL)),
        ],
        out_specs=[
            pl.BlockSpec((tm, D_MODEL), lambda i: (i, 0)),
            pl.BlockSpec((tm, D_MODEL), lambda i: (i, 0)),
        ],
        out_shape=[
            jax.ShapeDtypeStruct((n, D_MODEL), F32),
            jax.ShapeDtypeStruct((n, D_MODEL), BF16),
        ],
        compiler_params=_cparams("parallel"),
        name="merge",
    )(ua, ub, z, z, x2d, p["w_a"], p["w_b"], p["w_out"], p["g_ffn"])


def _cand_table():
    pairs = [(a, b) for a in range(PEER_TOPK) for b in range(PEER_TOPK)
             if (a + 1) * (b + 1) <= PEER_TOPK]
    return pairs


CAND_PAIRS = _cand_table()
N_CAND = len(CAND_PAIRS)
N_CAND_PAD = -(-N_CAND // SUBLANES) * SUBLANES
NEG_INF = float("-inf")
BIG = 1e9


def _top16_rows(s, vals_ref, idx_ref, lanes):
    rowf = lax.broadcasted_iota(jnp.int32, s.shape, 0).astype(F32)
    for k in range(PEER_TOPK):
        m = jnp.max(s, axis=0, keepdims=True)
        idx = jnp.min(jnp.where(s == m, rowf, BIG), axis=0, keepdims=True)
        vals_ref[k:k + 1, lanes] = m
        idx_ref[k:k + 1, lanes] = idx
        s = jnp.where(rowf == idx, NEG_INF, s)


def _select_kernel(xn_ref, wq_ref, k1_ref, k2_ref, flat_ref, i1o_ref, i2o_ref, go_ref,
                   q_s, v1_s, j1_s, v2_s, j2_s, cand_s, i1t_s, i2t_s, gt_s, *, tt_rows):
    q_s[...] = jnp.dot(xn_ref[...], wq_ref[...], preferred_element_type=F32).astype(BF16)
    nchunk = tt_rows // LANES
    flat = flat_ref[...]

    def head_body(h, carry):
        c0 = pl.multiple_of(h * 2 * D_HALF, 2 * D_HALF)
        q1 = q_s[:, pl.ds(c0, D_HALF)]
        q2 = q_s[:, pl.ds(c0 + D_HALF, D_HALF)]
        s1 = lax.dot_general(k1_ref[h], q1, NT_DIMS, preferred_element_type=F32)
        s2 = lax.dot_general(k2_ref[h], q2, NT_DIMS, preferred_element_type=F32)
        for c in range(nchunk):
            lanes = slice(c * LANES, (c + 1) * LANES)
            _top16_rows(s1[:, lanes], v1_s, j1_s, lanes)
            _top16_rows(s2[:, lanes], v2_s, j2_s, lanes)
            for r, (a, b) in enumerate(CAND_PAIRS):
                cand_s[r:r + 1, lanes] = v1_s[a:a + 1, lanes] + v2_s[b:b + 1, lanes]
            if N_CAND_PAD > N_CAND:
                cand_s[N_CAND:N_CAND_PAD, lanes] = jnp.full((N_CAND_PAD - N_CAND, LANES), NEG_INF, F32)
            cand = cand_s[:, lanes]
            svs, aks, bks = [], [], []
            for k in range(PEER_TOPK):
                m = jnp.max(cand, axis=0, keepdims=True)
                fi = jnp.min(jnp.where(cand == m, flat, BIG), axis=0, keepdims=True)
                ak = jnp.floor(fi * (1.0 / PEER_TOPK))
                svs.append(m)
                aks.append(ak)
                bks.append(fi - ak * PEER_TOPK)
                cand = jnp.where(flat == fi, NEG_INF, cand)
            sv = jnp.concatenate(svs, axis=0)
            ak = jnp.concatenate(aks, axis=0)
            bk = jnp.concatenate(bks, axis=0)
            i1sel = jnp.zeros((PEER_TOPK, LANES), F32)
            i2sel = jnp.zeros((PEER_TOPK, LANES), F32)
            for a in range(PEER_TOPK):
                i1sel = i1sel + jnp.where(ak == a, j1_s[a:a + 1, lanes], 0.0)
                i2sel = i2sel + jnp.where(bk == a, j2_s[a:a + 1, lanes], 0.0)
            ex = jnp.exp(sv - jnp.max(sv, axis=0, keepdims=True))
            g = ex / jnp.sum(ex, axis=0, keepdims=True)
            r0 = pl.multiple_of(h * PEER_TOPK, PEER_TOPK)
            i1t_s[pl.ds(r0, PEER_TOPK), lanes] = i1sel
            i2t_s[pl.ds(r0, PEER_TOPK), lanes] = i2sel
            gt_s[pl.ds(r0, PEER_TOPK), lanes] = g
        return carry

    lax.fori_loop(0, PEER_HEADS, head_body, 0)

    for c in range(nchunk):
        lanes = slice(c * LANES, (c + 1) * LANES)
        i1o_ref[lanes, :] = i1t_s[:, lanes].T
        i2o_ref[lanes, :] = i2t_s[:, lanes].T
        go_ref[lanes, :] = gt_s[:, lanes].T


def _select(xn2, p, flat_tab):
    n = xn2.shape[0]
    tt_rows = 256
    nsel = PEER_HEADS * PEER_TOPK
    const = lambda shape: pl.BlockSpec(shape, lambda i: (0,) * len(shape),
                                       pipeline_mode=pl.Buffered(1))
    out = jax.ShapeDtypeStruct((n, nsel), F32)
    ospec = pl.BlockSpec((tt_rows, nsel), lambda i: (i, 0))
    return pl.pallas_call(
        functools.partial(_select_kernel, tt_rows=tt_rows),
        grid=(n // tt_rows,),
        in_specs=[
            pl.BlockSpec((tt_rows, D_MODEL), lambda i: (i, 0)),
            const((D_MODEL, PEER_HEADS * 2 * D_HALF)),
            const((PEER_HEADS, N_KEYS, D_HALF)), const((PEER_HEADS, N_KEYS, D_HALF)),
            const((N_CAND_PAD, LANES)),
        ],
        out_specs=[ospec, ospec, ospec],
        out_shape=[out, out, out],
        scratch_shapes=[
            pltpu.VMEM((tt_rows, PEER_HEADS * 2 * D_HALF), BF16),
            pltpu.VMEM((PEER_TOPK, tt_rows), F32), pltpu.VMEM((PEER_TOPK, tt_rows), F32),
            pltpu.VMEM((PEER_TOPK, tt_rows), F32), pltpu.VMEM((PEER_TOPK, tt_rows), F32),
            pltpu.VMEM((N_CAND_PAD, tt_rows), F32),
            pltpu.VMEM((nsel, tt_rows), F32), pltpu.VMEM((nsel, tt_rows), F32),
            pltpu.VMEM((nsel, tt_rows), F32),
        ],
        compiler_params=_cparams("parallel"),
        name="peer_select",
    )(xn2, p["w_q"], p["keys1"], p["keys2"], flat_tab)


PEER_TB = 512
PEER_EB = 512
PEER_C = 64
PEER_PITCH = PEER_C + SUBLANES
I1_PER_STEP = PEER_EB // N_KEYS
STEPS_PER_BUILD = PEER_C // I1_PER_STEP


def _peer_dense_kernel(xn_ref, x1_ref, i1_ref, i2_ref, g_ref, u_ref, v_ref, gfin_ref, o_ref, g3_ref):
    s = pl.program_id(1)
    nsteps = pl.num_programs(1)
    phase = s % STEPS_PER_BUILD

    @pl.when(phase == 0)
    def _build():
        base = ((s // STEPS_PER_BUILD) * PEER_C).astype(F32)
        i1_rows = lax.broadcasted_iota(jnp.int32, (PEER_C, LANES), 0).astype(F32) + base
        i2_rows = lax.broadcasted_iota(jnp.int32, (N_KEYS, LANES), 0).astype(F32)

        def tok(t, carry):
            i1row = i1_ref[pl.ds(t, 1), :]
            i2row = i2_ref[pl.ds(t, 1), :]
            grow = g_ref[pl.ds(t, 1), :]
            at = jnp.where(i1_rows == i1row, 1.0, 0.0).astype(BF16)
            bt = jnp.where(i2_rows == i2row, grow, 0.0).astype(BF16)
            gt = lax.dot_general(at, bt, NT_DIMS, preferred_element_type=F32)
            g3_ref[pl.ds(pl.multiple_of(t * PEER_PITCH, SUBLANES), PEER_C), :] = gt
            return carry

        lax.fori_loop(0, PEER_TB, tok, 0, unroll=2)

    @pl.when(s == 0)
    def _():
        o_ref[...] = jnp.zeros_like(o_ref)

    sc = lax.dot_general(xn_ref[...], u_ref[...], NT_DIMS, preferred_element_type=F32)
    parts = [g3_ref[pl.ds(phase * I1_PER_STEP + k, PEER_TB, stride=PEER_PITCH), :]
             for k in range(I1_PER_STEP)]
    gate = jnp.concatenate(parts, axis=1)
    w2 = (_gelu(sc) * gate).astype(BF16)
    o_ref[...] += jnp.dot(w2, v_ref[...], preferred_element_type=F32)

    @pl.when(s == nsteps - 1)
    def _():
        x = x1_ref[...] + o_ref[...]
        ms = jnp.mean(x * x, axis=-1, keepdims=True)
        o_ref[...] = x * lax.rsqrt(ms + EPS) * gfin_ref[...]


def _peer_dense(xn2, x1, i1sel, i2sel, gsel, u_bf, v_bf, g_final):
    n = xn2.shape[0]
    nsel = PEER_HEADS * PEER_TOPK
    n_exp = u_bf.shape[0]
    tile = lambda w: pl.BlockSpec((PEER_TB, w), lambda i, s: (i, 0))
    return pl.pallas_call(
        _peer_dense_kernel,
        grid=(n // PEER_TB, n_exp // PEER_EB),
        in_specs=[
            tile(D_MODEL),
            pl.BlockSpec((PEER_TB, D_MODEL), lambda i, s: (i, 0), pipeline_mode=pl.Buffered(1)),
            tile(nsel), tile(nsel), tile(nsel),
            pl.BlockSpec((PEER_EB, D_MODEL), lambda i, s: (s, 0)),
            pl.BlockSpec((PEER_EB, D_MODEL), lambda i, s: (s, 0)),
            pl.BlockSpec((1, D_MODEL), lambda i, s: (0, 0)),
        ],
        out_specs=pl.BlockSpec((PEER_TB, D_MODEL), lambda i, s: (i, 0)),
        out_shape=jax.ShapeDtypeStruct((n, D_MODEL), F32),
        scratch_shapes=[pltpu.VMEM((PEER_TB * PEER_PITCH, LANES), F32)],
        compiler_params=_cparams("parallel", "arbitrary"),
        name="peer_dense",
    )(xn2, x1, i1sel, i2sel, gsel, u_bf, v_bf, g_final)


def _flat_table():
    tab = np.full((N_CAND_PAD, LANES), BIG, np.float32)
    for r, (a, b) in enumerate(CAND_PAIRS):
        tab[r, :] = a * PEER_TOPK + b
    return jnp.asarray(tab)


def _token_path(x2d, ua, ub, z, p, flat_tab):
    x1, xn2 = _merge(ua, ub, z, x2d, p)
    i1sel, i2sel, gsel = _select(xn2, p, flat_tab)
    return _peer_dense(xn2, x1, i1sel, i2sel, gsel, p["u_tab"], p["v_tab"], p["g_final"])


def kernel(x_prompt, x_sample, state_rnn_conv, state_rnn_h, state_cm_conv, g_mix, w_in, w_rconv,
           b_rconv, w_r, b_r, w_i, b_i, lam, w_a, w_cconv, b_cconv, ln_g, ln_b, w_b, w_out, g_ffn,
           w_q, keys1, keys2, u_tab, v_tab, g_final):
    assert w_in.shape[0] == 1, "single layer"
    bp, tp, d = x_prompt.shape
    bs, ts, _ = x_sample.shape
    row = lambda v: v.reshape(1, -1)
    p = {
        "g_mix": g_mix[0:1], "w_in": w_in[0].astype(BF16),
        "w_rconv": w_rconv[0], "b_rconv": b_rconv[0:1],
        "w_r": w_r[0].astype(BF16), "b_r": b_r[0:1], "w_i": w_i[0].astype(BF16), "b_i": b_i[0:1],
        "lam": lam[0:1], "w_a": w_a[0].astype(BF16),
        "w_cconv": w_cconv[0], "b_cconv": b_cconv[0:1], "ln_g": ln_g[0:1], "ln_b": ln_b[0:1],
        "w_b": w_b[0].astype(BF16), "w_out": w_out[0].astype(BF16), "g_ffn": g_ffn[0:1],
        "w_q": w_q[0].astype(BF16), "keys1": keys1[0].astype(BF16), "keys2": keys2[0].astype(BF16),
        "u_tab": u_tab[0].astype(BF16), "v_tab": v_tab[0].astype(BF16), "g_final": row(g_final),
    }
    flat_tab = _flat_table()

    xp2d = x_prompt.reshape(bp * tp, d)
    zp = _inproj(xp2d, p["g_mix"], p["w_in"])
    ua_p, rc_p, h_p = _rnn_prompt(zp, bp, tp, p)
    ub_p, cc_p = _conv_prompt(zp, bp, tp, p)
    y_p = _token_path(xp2d, ua_p, ub_p, zp, p, flat_tab).reshape(bp, tp, d)

    xs2d = jnp.transpose(x_sample, (1, 0, 2)).reshape(ts * bs, d)
    zs = _inproj(xs2d, p["g_mix"], p["w_in"])
    ua_s, ub_s, rc_s, h_s, cc_s = _seq_sample(
        zs.reshape(ts, bs, IN_COLS),
        state_rnn_conv[0].reshape(bs, -1), state_rnn_h[0], state_cm_conv[0].reshape(bs, -1), p)
    y_s = _token_path(xs2d, ua_s.reshape(ts * bs, -1), ub_s.reshape(ts * bs, -1), zs, p, flat_tab)
    y_s = jnp.transpose(y_s.reshape(ts, bs, d), (1, 0, 2))

    return (y_p, y_s,
            rc_p[None], h_p.reshape(1, bp, D_RNN), cc_p[None],
            rc_s.reshape(1, bs, RNN_CONV_W - 1, D_RNN), h_s[None],
            cc_s.reshape(1, bs, CONV_W - 1, D_CONV))
```

```python
import functools

import numpy as np
import jax
import jax.numpy as jnp
from jax import lax
from jax.experimental import pallas as pl
from jax.experimental.pallas import tpu as pltpu

F32 = jnp.float32
BF16 = jnp.bfloat16

D_MODEL = 2048
D_RNN = 1024
RNN_HEADS = 8
RNN_HEAD_DIM = 128
RNN_CONV_W = 4
LRU_C = 8.0
D_CONV = 1024
CONV_W = 31
N_KEYS = 128
PEER_HEADS = 8
PEER_TOPK = 16
D_HALF = 128
EPS = 1e-6
IN_COLS = 2 * D_RNN + 2 * D_CONV + 2 * D_MODEL

SUBLANES = 8
LANES = 128
VMEM_LIMIT = 56 * 1024 * 1024

NT_DIMS = (((1,), (1,)), ((), ()))


def _cparams(*sem):
    return pltpu.CompilerParams(dimension_semantics=sem, vmem_limit_bytes=VMEM_LIMIT)


def _gelu(x):
    return jax.nn.gelu(x)


def _sigmoid(x):
    return jax.nn.sigmoid(x)


def _inproj_kernel(x_ref, g_ref, w_ref, o_ref, xn_ref, *, rows):
    @pl.when(pl.program_id(1) == 0)
    def _():
        def body(i, c):
            r0 = pl.multiple_of(i * 16, 16)
            x = x_ref[pl.ds(r0, 16), :]
            ms = jnp.mean(x * x, axis=-1, keepdims=True)
            xn_ref[pl.ds(r0, 16), :] = (x * lax.rsqrt(ms + EPS) * g_ref[...]).astype(BF16)
            return c
        lax.fori_loop(0, rows // 16, body, 0)

    o_ref[...] = jnp.dot(xn_ref[...], w_ref[...], preferred_element_type=F32)


def _inproj(x2d, g_mix, w_in_bf):
    n = x2d.shape[0]
    tm = min(n, 1024)
    tn = 1024
    return pl.pallas_call(
        functools.partial(_inproj_kernel, rows=tm),
        grid=(n // tm, IN_COLS // tn),
        in_specs=[
            pl.BlockSpec((tm, D_MODEL), lambda i, j: (i, 0)),
            pl.BlockSpec((1, D_MODEL), lambda i, j: (0, 0)),
            pl.BlockSpec((D_MODEL, tn), lambda i, j: (0, j)),
        ],
        out_specs=pl.BlockSpec((tm, tn), lambda i, j: (i, j)),
        out_shape=jax.ShapeDtypeStruct((n, IN_COLS), F32),
        scratch_shapes=[pltpu.VMEM((tm, D_MODEL), BF16)],
        compiler_params=_cparams("parallel", "arbitrary"),
        name="inproj",
    )(x2d, g_mix, w_in_bf)


def _lru_gates(xc, wr_ref, br_ref, wi_ref, bi_ref, lam_ref):
    nl = -lam_ref[...]
    c = -LRU_C * (jnp.maximum(nl, 0.0) + jnp.log1p(jnp.exp(-jnp.abs(nl))))
    a_parts, b_parts = [], []
    for h in range(RNN_HEADS):
        sl = slice(h * RNN_HEAD_DIM, (h + 1) * RNN_HEAD_DIM)
        xh = xc[:, sl]
        xb = xh.astype(BF16)
        r = _sigmoid(jnp.dot(xb, wr_ref[h], preferred_element_type=F32) + br_ref[:, sl])
        ig = _sigmoid(jnp.dot(xb, wi_ref[h], preferred_element_type=F32) + bi_ref[:, sl])
        log_a = c[:, sl] * r
        a = jnp.exp(log_a)
        a_parts.append(a)
        b_parts.append(jnp.sqrt(-jnp.tanh(log_a) * (a * a + 1.0)) * (ig * xh))
    return jnp.concatenate(a_parts, axis=1), jnp.concatenate(b_parts, axis=1)


def _rnn_prompt_kernel(xr_ref, yr_ref, wc_ref, bc_ref, wr_ref, br_ref, wi_ref, bi_ref, lam_ref,
                       ua_ref, rc_ref, hl_ref, xbuf, a_s, b_s, hcar, *, tt_rows, chunk):
    tt = pl.program_id(1)
    ntt = pl.num_programs(1)

    @pl.when(tt == 0)
    def _():
        xbuf[0:SUBLANES, :] = jnp.zeros((SUBLANES, D_RNN), F32)
        hcar[...] = jnp.zeros((1, D_RNN), F32)

    xbuf[SUBLANES:, :] = xr_ref[...]

    for ci in range(tt_rows // chunk):
        r0 = ci * chunk
        xc = jnp.broadcast_to(bc_ref[...], (chunk, D_RNN))
        for k in range(RNN_CONV_W):
            off = SUBLANES - (RNN_CONV_W - 1) + k + r0
            xc = xc + wc_ref[k:k + 1, :] * xbuf[off:off + chunk, :]
        a, b = _lru_gates(xc, wr_ref, br_ref, wi_ref, bi_ref, lam_ref)
        a_s[r0:r0 + chunk, :] = a
        b_s[r0:r0 + chunk, :] = b

    row = lax.broadcasted_iota(jnp.int32, (SUBLANES, D_RNN), 0)

    def scan_body(i, hprev):
        base = pl.multiple_of(i * 16, 16)
        hs = []
        for half in range(2):
            r0 = base + half * SUBLANES
            a = a_s[pl.ds(r0, SUBLANES), :]
            b = b_s[pl.ds(r0, SUBLANES), :]
            for s in (1, 2, 4):
                a_sh = jnp.where(row >= s, pltpu.roll(a, s, 0), 1.0)
                b_sh = jnp.where(row >= s, pltpu.roll(b, s, 0), 0.0)
                b = a * b_sh + b
                a = a * a_sh
            h = a * hprev + b
            hprev = h[SUBLANES - 1:SUBLANES, :]
            hs.append(h)
        h16 = jnp.concatenate(hs, axis=0)
        ua_ref[pl.ds(base, 16), :] = (h16 * _gelu(yr_ref[pl.ds(base, 16), :])).astype(BF16)
        return hprev

    hlast = lax.fori_loop(0, tt_rows // 16, scan_body, hcar[...])
    hcar[...] = hlast
    xbuf[0:SUBLANES, :] = xbuf[tt_rows:tt_rows + SUBLANES, :]

    @pl.when(tt == ntt - 1)
    def _():
        rc_ref[0] = xbuf[SUBLANES - (RNN_CONV_W - 1):SUBLANES, :]
        hl_ref[0] = hlast


def _rnn_prompt(z, nb, t, p):
    tt_rows = 256
    ntt = t // tt_rows
    full = lambda shape: pl.BlockSpec(shape, lambda b, i: (0,) * len(shape))
    return pl.pallas_call(
        functools.partial(_rnn_prompt_kernel, tt_rows=tt_rows, chunk=128),
        grid=(nb, ntt),
        in_specs=[
            pl.BlockSpec((tt_rows, D_RNN), lambda b, i: (b * ntt + i, 0)),
            pl.BlockSpec((tt_rows, D_RNN), lambda b, i: (b * ntt + i, 1)),
            full((RNN_CONV_W, D_RNN)), full((1, D_RNN)),
            full((RNN_HEADS, RNN_HEAD_DIM, RNN_HEAD_DIM)), full((1, D_RNN)),
            full((RNN_HEADS, RNN_HEAD_DIM, RNN_HEAD_DIM)), full((1, D_RNN)),
            full((1, D_RNN)),
        ],
        out_specs=[
            pl.BlockSpec((tt_rows, D_RNN), lambda b, i: (b * ntt + i, 0)),
            pl.BlockSpec((1, RNN_CONV_W - 1, D_RNN), lambda b, i: (b, 0, 0)),
            pl.BlockSpec((1, 1, D_RNN), lambda b, i: (b, 0, 0)),
        ],
        out_shape=[
            jax.ShapeDtypeStruct((nb * t, D_RNN), BF16),
            jax.ShapeDtypeStruct((nb, RNN_CONV_W - 1, D_RNN), F32),
            jax.ShapeDtypeStruct((nb, 1, D_RNN), F32),
        ],
        scratch_shapes=[
            pltpu.VMEM((tt_rows + SUBLANES, D_RNN), F32),
            pltpu.VMEM((tt_rows, D_RNN), F32),
            pltpu.VMEM((tt_rows, D_RNN), F32),
            pltpu.VMEM((1, D_RNN), F32),
        ],
        compiler_params=_cparams("arbitrary", "arbitrary"),
        name="rnn_prompt",
    )(z, z, p["w_rconv"], p["b_rconv"], p["w_r"], p["b_r"], p["w_i"], p["b_i"], p["lam"])


CONV_PAD = 32


def _layernorm_silu(dc, g, b):
    mu = jnp.mean(dc, axis=-1, keepdims=True)
    xc = dc - mu
    var = jnp.mean(xc * xc, axis=-1, keepdims=True)
    y = xc * lax.rsqrt(var + EPS) * g + b
    return y * _sigmoid(y)


def _conv_prompt_kernel(ca_ref, cb_ref, wc_ref, bc_ref, lg_ref, lb_ref, ub_ref, cc_ref, gbuf, w8, dc,
                        *, tt_rows, chunk):
    tt = pl.program_id(1)
    ntt = pl.num_programs(1)

    @pl.when(tt == 0)
    def _():
        for r in range(SUBLANES):
            gbuf[r, 0:CONV_PAD, :] = jnp.zeros((CONV_PAD, D_CONV), F32)
        for k in range(CONV_W):
            w8[k] = jnp.broadcast_to(wc_ref[k:k + 1, :], (SUBLANES, D_CONV))

    def glu_body(i, c):
        r0 = pl.multiple_of(i * chunk, chunk)
        gbuf[0, pl.ds(CONV_PAD + r0, chunk), :] = (
            ca_ref[pl.ds(r0, chunk), :] * _sigmoid(cb_ref[pl.ds(r0, chunk), :]))
        return c
    lax.fori_loop(0, tt_rows // chunk, glu_body, 0)

    for r in range(1, SUBLANES):
        for ci in range(tt_rows // 64):
            r0 = CONV_PAD + ci * 64
            gbuf[r, r0:r0 + 64, :] = gbuf[0, r0 - r:r0 - r + 64, :]

    groups = tt_rows // SUBLANES

    def conv_body(i, c):
        l0 = pl.multiple_of(i * LANES, LANES)
        acc = jnp.broadcast_to(bc_ref[:, pl.ds(l0, LANES)], (groups, SUBLANES, LANES))
        for k in range(CONV_W):
            lag = CONV_W - 1 - k
            q, r = divmod(lag, SUBLANES)
            x = gbuf[r, CONV_PAD - q * SUBLANES:CONV_PAD - q * SUBLANES + tt_rows, pl.ds(l0, LANES)]
            acc = acc + w8[k, :, pl.ds(l0, LANES)][None] * x.reshape(groups, SUBLANES, LANES)
        dc[:, pl.ds(l0, LANES)] = acc.reshape(tt_rows, LANES)
        return c
    lax.fori_loop(0, D_CONV // LANES, conv_body, 0)

    nrows = 4 * chunk

    def norm_body(i, c):
        r0 = pl.multiple_of(i * nrows, nrows)
        ub_ref[pl.ds(r0, nrows), :] = _layernorm_silu(
            dc[pl.ds(r0, nrows), :], lg_ref[...], lb_ref[...]).astype(BF16)
        return c
    lax.fori_loop(0, tt_rows // nrows, norm_body, 0)

    @pl.when(tt == ntt - 1)
    def _():
        cc_ref[0] = gbuf[0, tt_rows + CONV_PAD - (CONV_W - 1):tt_rows + CONV_PAD, :]

    for r in range(SUBLANES):
        gbuf[r, 0:CONV_PAD, :] = gbuf[r, tt_rows:tt_rows + CONV_PAD, :]


def _conv_prompt(z, nb, t, p):
    tt_rows = 256
    ntt = t // tt_rows
    full = lambda shape: pl.BlockSpec(shape, lambda b, i: (0,) * len(shape))
    return pl.pallas_call(
        functools.partial(_conv_prompt_kernel, tt_rows=tt_rows, chunk=32),
        grid=(nb, ntt),
        in_specs=[
            pl.BlockSpec((tt_rows, D_CONV), lambda b, i: (b * ntt + i, 2)),
            pl.BlockSpec((tt_rows, D_CONV), lambda b, i: (b * ntt + i, 3)),
            full((CONV_W, D_CONV)), full((1, D_CONV)), full((1, D_CONV)), full((1, D_CONV)),
        ],
        out_specs=[
            pl.BlockSpec((tt_rows, D_CONV), lambda b, i: (b * ntt + i, 0)),
            pl.BlockSpec((1, CONV_W - 1, D_CONV), lambda b, i: (b, 0, 0)),
        ],
        out_shape=[
            jax.ShapeDtypeStruct((nb * t, D_CONV), BF16),
            jax.ShapeDtypeStruct((nb, CONV_W - 1, D_CONV), F32),
        ],
        scratch_shapes=[pltpu.VMEM((SUBLANES, tt_rows + CONV_PAD, D_CONV), F32),
                        pltpu.VMEM((CONV_W, SUBLANES, D_CONV), F32),
                        pltpu.VMEM((tt_rows, D_CONV), F32)],
        compiler_params=_cparams("arbitrary", "arbitrary"),
        name="conv_prompt",
    )(z, z, p["w_cconv"], p["b_cconv"], p["ln_g"], p["ln_b"])


def _seq_sample_kernel(xr_ref, yr_ref, ca_ref, cb_ref, rc0_ref, h0_ref, cc0_ref,
                       wc_ref, bc_ref, wr_ref, br_ref, wi_ref, bi_ref, lam_ref,
                       wcc_ref, bcc_ref, lg_ref, lb_ref,
                       ua_ref, ub_ref, rc_ref, hl_ref, cc_ref, *, steps, sb):
    nrc = RNN_CONV_W - 1
    ncc = CONV_W - 1

    def xp(j):
        if j < nrc:
            return rc0_ref[:, j * D_RNN:(j + 1) * D_RNN]
        return xr_ref[j - nrc]

    h = h0_ref[...]
    for t in range(steps):
        xc = jnp.broadcast_to(bc_ref[...], (sb, D_RNN))
        for k in range(RNN_CONV_W):
            xc = xc + wc_ref[k:k + 1, :] * xp(t + k)
        a, b = _lru_gates(xc, wr_ref, br_ref, wi_ref, bi_ref, lam_ref)
        h = a * h + b
        ua_ref[t] = (h * _gelu(yr_ref[t])).astype(BF16)
    hl_ref[...] = h
    for j in range(nrc):
        rc_ref[:, j * D_RNN:(j + 1) * D_RNN] = xp(steps + j)

    for j in range(ncc - steps):
        cc_ref[:, j * D_CONV:(j + 1) * D_CONV] = cc0_ref[:, (j + steps) * D_CONV:(j + steps + 1) * D_CONV]
    for t in range(steps):
        j = ncc - steps + t
        cc_ref[:, j * D_CONV:(j + 1) * D_CONV] = ca_ref[t] * _sigmoid(cb_ref[t])

    def gp(j):
        if j < steps:
            return cc0_ref[:, j * D_CONV:(j + 1) * D_CONV]
        return cc_ref[:, (j - steps) * D_CONV:(j - steps + 1) * D_CONV]

    for t in range(steps):
        acc = jnp.broadcast_to(bcc_ref[...], (sb, D_CONV))
        for k in range(CONV_W):
            acc = acc + wcc_ref[k:k + 1, :] * gp(t + k)
        ub_ref[t] = _layernorm_silu(acc, lg_ref[...], lb_ref[...]).astype(BF16)


def _seq_sample(z3, rc0, h0, cc0, p):
    steps, nseq, _ = z3.shape
    sb = 32
    full = lambda shape: pl.BlockSpec(shape, lambda i: (0,) * len(shape))
    zspec = lambda col: pl.BlockSpec((steps, sb, D_RNN), lambda i, col=col: (0, i, col))
    nrc, ncc = RNN_CONV_W - 1, CONV_W - 1
    return pl.pallas_call(
        functools.partial(_seq_sample_kernel, steps=steps, sb=sb),
        grid=(nseq // sb,),
        in_specs=[
            zspec(0), zspec(1), zspec(2), zspec(3),
            pl.BlockSpec((sb, nrc * D_RNN), lambda i: (i, 0)),
            pl.BlockSpec((sb, D_RNN), lambda i: (i, 0)),
            pl.BlockSpec((sb, ncc * D_CONV), lambda i: (i, 0)),
            full((RNN_CONV_W, D_RNN)), full((1, D_RNN)),
            full((RNN_HEADS, RNN_HEAD_DIM, RNN_HEAD_DIM)), full((1, D_RNN)),
            full((RNN_HEADS, RNN_HEAD_DIM, RNN_HEAD_DIM)), full((1, D_RNN)),
            full((1, D_RNN)),
            full((CONV_W, D_CONV)), full((1, D_CONV)), full((1, D_CONV)), full((1, D_CONV)),
        ],
        out_specs=[
            pl.BlockSpec((steps, sb, D_RNN), lambda i: (0, i, 0)),
            pl.BlockSpec((steps, sb, D_CONV), lambda i: (0, i, 0)),
            pl.BlockSpec((sb, nrc * D_RNN), lambda i: (i, 0)),
            pl.BlockSpec((sb, D_RNN), lambda i: (i, 0)),
            pl.BlockSpec((sb, ncc * D_CONV), lambda i: (i, 0)),
        ],
        out_shape=[
            jax.ShapeDtypeStruct((steps, nseq, D_RNN), BF16),
            jax.ShapeDtypeStruct((steps, nseq, D_CONV), BF16),
            jax.ShapeDtypeStruct((nseq, nrc * D_RNN), F32),
            jax.ShapeDtypeStruct((nseq, D_RNN), F32),
            jax.ShapeDtypeStruct((nseq, ncc * D_CONV), F32),
        ],
        compiler_params=_cparams("parallel"),
        name="seq_sample",
    )(z3, z3, z3, z3, rc0, h0, cc0,
      p["w_rconv"], p["b_rconv"], p["w_r"], p["b_r"], p["w_i"], p["b_i"], p["lam"],
      p["w_cconv"], p["b_cconv"], p["ln_g"], p["ln_b"])


def _merge_kernel(ua_ref, ub_ref, ga_ref, gb_ref, x_ref, wa_ref, wb_ref, wo_ref, gf_ref,
                  x1_ref, xn_ref):
    ba = jnp.dot(ua_ref[...], wa_ref[...], preferred_element_type=F32)
    bb = jnp.dot(ub_ref[...], wb_ref[...], preferred_element_type=F32)
    merged = _sigmoid(ga_ref[...]) * ba + _sigmoid(gb_ref[...]) * bb
    x1 = x_ref[...] + jnp.dot(merged.astype(BF16), wo_ref[...], preferred_element_type=F32)
    x1_ref[...] = x1
    ms = jnp.mean(x1 * x1, axis=-1, keepdims=True)
    xn_ref[...] = (x1 * lax.rsqrt(ms + EPS) * gf_ref[...]).astype(BF16)


def _merge(ua, ub, z, x2d, p):
    n = x2d.shape[0]
    tm = 256
    const = lambda shape: pl.BlockSpec(shape, lambda i: (0,) * len(shape),
                                       pipeline_mode=pl.Buffered(1))
    return pl.pallas_call(
        _merge_kernel,
        grid=(n // tm,),
        in_specs=[
            pl.BlockSpec((tm, D_RNN), lambda i: (i, 0)),
            pl.BlockSpec((tm, D_CONV), lambda i: (i, 0)),
            pl.BlockSpec((tm, D_MODEL), lambda i: (i, 2)),
            pl.BlockSpec((tm, D_MODEL), lambda i: (i, 3)),
            pl.BlockSpec((tm, D_MODEL), lambda i: (i, 0)),
            const((D_RNN, D_MODEL)), const((D_CONV, D_MODEL)), const((D_MODEL, D_MODEL)),
            const((1, D_MODEL)),
        ],
        out_specs=[
            pl.BlockSpec((tm, D_MODEL), lambda i: (i, 0)),
            pl.BlockSpec((tm, D_MODEL), lambda i: (i, 0)),
        ],
        out_shape=[
            jax.ShapeDtypeStruct((n, D_MODEL), F32),
            jax.ShapeDtypeStruct((n, D_MODEL), BF16),
        ],
        compiler_params=_cparams("parallel"),
        name="merge",
    )(ua, ub, z, z, x2d, p["w_a"], p["w_b"], p["w_out"], p["g_ffn"])


def _cand_table():
    pairs = [(a, b) for a in range(PEER_TOPK) for b in range(PEER_TOPK)
             if (a + 1) * (b + 1) <= PEER_TOPK]
    return pairs


CAND_PAIRS = _cand_table()
N_CAND = len(CAND_PAIRS)
N_CAND_PAD = -(-N_CAND // SUBLANES) * SUBLANES
NEG_INF = float("-inf")
BIG = 1e9


def _top16_rows(s, vals_ref, idx_ref, lanes):
    rowf = lax.broadcasted_iota(jnp.int32, s.shape, 0).astype(F32)
    for k in range(PEER_TOPK):
        m = jnp.max(s, axis=0, keepdims=True)
        idx = jnp.min(jnp.where(s == m, rowf, BIG), axis=0, keepdims=True)
        vals_ref[k:k + 1, lanes] = m
        idx_ref[k:k + 1, lanes] = idx
        s = jnp.where(rowf == idx, NEG_INF, s)


def _select_kernel(xn_ref, wq_ref, k1_ref, k2_ref, flat_ref, i1o_ref, i2o_ref, go_ref,
                   q_s, v1_s, j1_s, v2_s, j2_s, cand_s, i1t_s, i2t_s, gt_s, *, tt_rows):
    q_s[...] = jnp.dot(xn_ref[...], wq_ref[...], preferred_element_type=F32).astype(BF16)
    nchunk = tt_rows // LANES
    flat = flat_ref[...]

    def head_body(h, carry):
        c0 = pl.multiple_of(h * 2 * D_HALF, 2 * D_HALF)
        q1 = q_s[:, pl.ds(c0, D_HALF)]
        q2 = q_s[:, pl.ds(c0 + D_HALF, D_HALF)]
        s1 = lax.dot_general(k1_ref[h], q1, NT_DIMS, preferred_element_type=F32)
        s2 = lax.dot_general(k2_ref[h], q2, NT_DIMS, preferred_element_type=F32)
        for c in range(nchunk):
            lanes = slice(c * LANES, (c + 1) * LANES)
            _top16_rows(s1[:, lanes], v1_s, j1_s, lanes)
            _top16_rows(s2[:, lanes], v2_s, j2_s, lanes)
            for r, (a, b) in enumerate(CAND_PAIRS):
                cand_s[r:r + 1, lanes] = v1_s[a:a + 1, lanes] + v2_s[b:b + 1, lanes]
            if N_CAND_PAD > N_CAND:
                cand_s[N_CAND:N_CAND_PAD, lanes] = jnp.full((N_CAND_PAD - N_CAND, LANES), NEG_INF, F32)
            cand = cand_s[:, lanes]
            svs, aks, bks = [], [], []
            for k in range(PEER_TOPK):
                m = jnp.max(cand, axis=0, keepdims=True)
                fi = jnp.min(jnp.where(cand == m, flat, BIG), axis=0, keepdims=True)
                ak = jnp.floor(fi * (1.0 / PEER_TOPK))
                svs.append(m)
                aks.append(ak)
                bks.append(fi - ak * PEER_TOPK)
                cand = jnp.where(flat == fi, NEG_INF, cand)
            sv = jnp.concatenate(svs, axis=0)
            ak = jnp.concatenate(aks, axis=0)
            bk = jnp.concatenate(bks, axis=0)
            i1sel = jnp.zeros((PEER_TOPK, LANES), F32)
            i2sel = jnp.zeros((PEER_TOPK, LANES), F32)
            for a in range(PEER_TOPK):
                i1sel = i1sel + jnp.where(ak == a, j1_s[a:a + 1, lanes], 0.0)
                i2sel = i2sel + jnp.where(bk == a, j2_s[a:a + 1, lanes], 0.0)
            ex = jnp.exp(sv - jnp.max(sv, axis=0, keepdims=True))
            g = ex / jnp.sum(ex, axis=0, keepdims=True)
            r0 = pl.multiple_of(h * PEER_TOPK, PEER_TOPK)
            i1t_s[pl.ds(r0, PEER_TOPK), lanes] = i1sel
            i2t_s[pl.ds(r0, PEER_TOPK), lanes] = i2sel
            gt_s[pl.ds(r0, PEER_TOPK), lanes] = g
        return carry

    lax.fori_loop(0, PEER_HEADS, head_body, 0)

    for c in range(nchunk):
        lanes = slice(c * LANES, (c + 1) * LANES)
        i1o_ref[lanes, :] = i1t_s[:, lanes].T
        i2o_ref[lanes, :] = i2t_s[:, lanes].T
        go_ref[lanes, :] = gt_s[:, lanes].T


def _select(xn2, p, flat_tab):
    n = xn2.shape[0]
    tt_rows = 256
    nsel = PEER_HEADS * PEER_TOPK
    const = lambda shape: pl.BlockSpec(shape, lambda i: (0,) * len(shape),
                                       pipeline_mode=pl.Buffered(1))
    out = jax.ShapeDtypeStruct((n, nsel), F32)
    ospec = pl.BlockSpec((tt_rows, nsel), lambda i: (i, 0))
    return pl.pallas_call(
        functools.partial(_select_kernel, tt_rows=tt_rows),
        grid=(n // tt_rows,),
        in_specs=[
            pl.BlockSpec((tt_rows, D_MODEL), lambda i: (i, 0)),
            const((D_MODEL, PEER_HEADS * 2 * D_HALF)),
            const((PEER_HEADS, N_KEYS, D_HALF)), const((PEER_HEADS, N_KEYS, D_HALF)),
            const((N_CAND_PAD, LANES)),
        ],
        out_specs=[ospec, ospec, ospec],
        out_shape=[out, out, out],
        scratch_shapes=[
            pltpu.VMEM((tt_rows, PEER_HEADS * 2 * D_HALF), BF16),
            pltpu.VMEM((PEER_TOPK, tt_rows), F32), pltpu.VMEM((PEER_TOPK, tt_rows), F32),
            pltpu.VMEM((PEER_TOPK, tt_rows), F32), pltpu.VMEM((PEER_TOPK, tt_rows), F32),
            pltpu.VMEM((N_CAND_PAD, tt_rows), F32),
            pltpu.VMEM((nsel, tt_rows), F32), pltpu.VMEM((nsel, tt_rows), F32),
            pltpu.VMEM((nsel, tt_rows), F32),
        ],
        compiler_params=_cparams("parallel"),
        name="peer_select",
    )(xn2, p["w_q"], p["keys1"], p["keys2"], flat_tab)


PEER_TB = 512
PEER_EB = 512
PEER_C = 64
PEER_PITCH = PEER_C + SUBLANES
I1_PER_STEP = PEER_EB // N_KEYS
STEPS_PER_BUILD = PEER_C // I1_PER_STEP
BUILD_GROUP = 16


def _peer_dense_kernel(xn_ref, x1_ref, i1_ref, i2_ref, g_ref, u_ref, v_ref, gfin_ref, o_ref, g3_ref):
    s = pl.program_id(1)
    nsteps = pl.num_programs(1)
    phase = s % STEPS_PER_BUILD

    @pl.when(phase == 0)
    def _build():
        base = ((s // STEPS_PER_BUILD) * PEER_C).astype(F32)
        i1_rows = lax.broadcasted_iota(jnp.int32, (PEER_C, LANES), 0).astype(F32) + base
        i2_rows = lax.broadcasted_iota(jnp.int32, (N_KEYS, LANES), 0).astype(F32)

        def tok_group(i, carry):
            t0 = pl.multiple_of(i * BUILD_GROUP, BUILD_GROUP)
            for j in range(BUILD_GROUP):
                t = t0 + j
                i1row = i1_ref[pl.ds(t, 1), :]
                i2row = i2_ref[pl.ds(t, 1), :]
                grow = g_ref[pl.ds(t, 1), :]
                at = jnp.where(i1_rows == i1row, 1.0, 0.0).astype(BF16)
                bt = jnp.where(i2_rows == i2row, grow, 0.0).astype(BF16)
                gt = lax.dot_general(at, bt, NT_DIMS, preferred_element_type=F32)
                g3_ref[pl.ds(pl.multiple_of(t * PEER_PITCH, SUBLANES), PEER_C), :] = gt
            return carry

        lax.fori_loop(0, PEER_TB // BUILD_GROUP, tok_group, 0)

    @pl.when(s == 0)
    def _():
        o_ref[...] = jnp.zeros_like(o_ref)

    sc = lax.dot_general(xn_ref[...], u_ref[...], NT_DIMS, preferred_element_type=F32)
    parts = [g3_ref[pl.ds(phase * I1_PER_STEP + k, PEER_TB, stride=PEER_PITCH), :]
             for k in range(I1_PER_STEP)]
    gate = jnp.concatenate(parts, axis=1)
    w2 = (_gelu(sc) * gate).astype(BF16)
    o_ref[...] += jnp.dot(w2, v_ref[...], preferred_element_type=F32)

    @pl.when(s == nsteps - 1)
    def _():
        x = x1_ref[...] + o_ref[...]
        ms = jnp.mean(x * x, axis=-1, keepdims=True)
        o_ref[...] = x * lax.rsqrt(ms + EPS) * gfin_ref[...]


def _peer_dense(xn2, x1, i1sel, i2sel, gsel, u_bf, v_bf, g_final):
    n = xn2.shape[0]
    nsel = PEER_HEADS * PEER_TOPK
    n_exp = u_bf.shape[0]
    tile = lambda w: pl.BlockSpec((PEER_TB, w), lambda i, s: (i, 0))
    return pl.pallas_call(
        _peer_dense_kernel,
        grid=(n // PEER_TB, n_exp // PEER_EB),
        in_specs=[
            tile(D_MODEL),
            pl.BlockSpec((PEER_TB, D_MODEL), lambda i, s: (i, 0), pipeline_mode=pl.Buffered(1)),
            tile(nsel), tile(nsel), tile(nsel),
            pl.BlockSpec((PEER_EB, D_MODEL), lambda i, s: (s, 0)),
            pl.BlockSpec((PEER_EB, D_MODEL), lambda i, s: (s, 0)),
            pl.BlockSpec((1, D_MODEL), lambda i, s: (0, 0)),
        ],
        out_specs=pl.BlockSpec((PEER_TB, D_MODEL), lambda i, s: (i, 0)),
        out_shape=jax.ShapeDtypeStruct((n, D_MODEL), F32),
        scratch_shapes=[pltpu.VMEM((PEER_TB * PEER_PITCH, LANES), F32)],
        compiler_params=_cparams("parallel", "arbitrary"),
        name="peer_dense",
    )(xn2, x1, i1sel, i2sel, gsel, u_bf, v_bf, g_final)


def _flat_table():
    tab = np.full((N_CAND_PAD, LANES), BIG, np.float32)
    for r, (a, b) in enumerate(CAND_PAIRS):
        tab[r, :] = a * PEER_TOPK + b
    return jnp.asarray(tab)


def _token_path(x2d, ua, ub, z, p, flat_tab):
    x1, xn2 = _merge(ua, ub, z, x2d, p)
    i1sel, i2sel, gsel = _select(xn2, p, flat_tab)
    return _peer_dense(xn2, x1, i1sel, i2sel, gsel, p["u_tab"], p["v_tab"], p["g_final"])


def kernel(x_prompt, x_sample, state_rnn_conv, state_rnn_h, state_cm_conv, g_mix, w_in, w_rconv,
           b_rconv, w_r, b_r, w_i, b_i, lam, w_a, w_cconv, b_cconv, ln_g, ln_b, w_b, w_out, g_ffn,
           w_q, keys1, keys2, u_tab, v_tab, g_final):
    assert w_in.shape[0] == 1, "single layer"
    bp, tp, d = x_prompt.shape
    bs, ts, _ = x_sample.shape
    row = lambda v: v.reshape(1, -1)
    p = {
        "g_mix": g_mix[0:1], "w_in": w_in[0].astype(BF16),
        "w_rconv": w_rconv[0], "b_rconv": b_rconv[0:1],
        "w_r": w_r[0].astype(BF16), "b_r": b_r[0:1], "w_i": w_i[0].astype(BF16), "b_i": b_i[0:1],
        "lam": lam[0:1], "w_a": w_a[0].astype(BF16),
        "w_cconv": w_cconv[0], "b_cconv": b_cconv[0:1], "ln_g": ln_g[0:1], "ln_b": ln_b[0:1],
        "w_b": w_b[0].astype(BF16), "w_out": w_out[0].astype(BF16), "g_ffn": g_ffn[0:1],
        "w_q": w_q[0].astype(BF16), "keys1": keys1[0].astype(BF16), "keys2": keys2[0].astype(BF16),
        "u_tab": u_tab[0].astype(BF16), "v_tab": v_tab[0].astype(BF16), "g_final": row(g_final),
    }
    flat_tab = _flat_table()

    xp2d = x_prompt.reshape(bp * tp, d)
    zp = _inproj(xp2d, p["g_mix"], p["w_in"])
    ua_p, rc_p, h_p = _rnn_prompt(zp, bp, tp, p)
    ub_p, cc_p = _conv_prompt(zp, bp, tp, p)
    y_p = _token_path(xp2d, ua_p, ub_p, zp, p, flat_tab).reshape(bp, tp, d)

    xs2d = jnp.transpose(x_sample, (1, 0, 2)).reshape(ts * bs, d)
    zs = _inproj(xs2d, p["g_mix"], p["w_in"])
    ua_s, ub_s, rc_s, h_s, cc_s = _seq_sample(
        zs.reshape(ts, bs, IN_COLS),
        state_rnn_conv[0].reshape(bs, -1), state_rnn_h[0], state_cm_conv[0].reshape(bs, -1), p)
    y_s = _token_path(xs2d, ua_s.reshape(ts * bs, -1), ub_s.reshape(ts * bs, -1), zs, p, flat_tab)
    y_s = jnp.transpose(y_s.reshape(ts, bs, d), (1, 0, 2))

    return (y_p, y_s,
            rc_p[None], h_p.reshape(1, bp, D_RNN), cc_p[None],
            rc_s.reshape(1, bs, RNN_CONV_W - 1, D_RNN), h_s[None],
            cc_s.reshape(1, bs, CONV_W - 1, D_CONV))
```
